```python
import jax, jax.numpy as jnp
from jax import lax
import numpy as np

D_MODEL = 2048
BATCH = 4
SEQ = 8192
DEPTH = 1

GRID_W = 64
WIN_R = 8
WIN_C = 16
QCB = 16
NCB = GRID_W // QCB
KCW = QCB + WIN_C
N_ATT_HEADS = 8
ATT_HEAD_DIM = 128
ATT_WIDTH = N_ATT_HEADS * ATT_HEAD_DIM
N_DN_HEADS = 8
DN_HEAD_DIM = 128
DN_KEY_WIDTH = N_DN_HEADS * DN_HEAD_DIM
DN_VAL_WIDTH = N_DN_HEADS * DN_HEAD_DIM
DN_QKV_WIDTH = 2 * DN_KEY_WIDTH + DN_VAL_WIDTH
CONV_K = 5
CHUNK = 64
N_DIR = 2
IN_SPLITS = (3 * ATT_WIDTH, DN_QKV_WIDTH, DN_VAL_WIDTH, N_DIR * N_DN_HEADS, N_DIR * N_DN_HEADS, D_MODEL, D_MODEL)
IN_WIDTH = sum(IN_SPLITS)
D_FF = 4 * D_MODEL
RMS_EPS = 1e-6
L2_EPS = 1e-6
NEG_INF = -1e30

kernel_name = "hybrid_natten_gdn_encoder_block"


def _rmsnorm(x, w):
    xf = x.astype(jnp.float32)
    y = xf * lax.rsqrt(jnp.mean(xf * xf, axis=-1, keepdims=True) + RMS_EPS)
    return (y * w.astype(jnp.float32)).astype(x.dtype)


def _l2norm(x):
    return x * lax.rsqrt(jnp.sum(x * x, axis=-1, keepdims=True) + L2_EPS)


def _window_tables(wr):
    qcol = np.arange(GRID_W).reshape(NCB, QCB)
    cs = np.clip(qcol - WIN_C // 2, 0, GRID_W - WIN_C)
    kcs = np.clip(np.arange(NCB) * QCB - WIN_C // 2, 0, GRID_W - KCW)
    col_idx = kcs[:, None] + np.arange(KCW)[None, :]
    kk = np.arange(wr * KCW)
    key_row = kk // KCW
    key_col = col_idx[:, kk % KCW]
    kc = key_col[:, None, :]
    col_mask = (kc >= cs[..., None]) & (kc < cs[..., None] + WIN_C)
    dc_idx = np.clip(kc - qcol[..., None] + WIN_C - 1, 0, 2 * WIN_C - 2)
    return col_idx, key_row, col_mask, dc_idx


def _neighbourhood_attention(q, k, v, rpb):
    b, s, h, dh = q.shape
    rows = s // GRID_W
    wr = min(WIN_R, rows)
    col_idx, key_row, col_mask, dc_idx = _window_tables(wr)
    key_row = jnp.asarray(key_row, jnp.int32)
    dc_idx = jnp.asarray(dc_idx, jnp.int32)
    rpb_flat = rpb.reshape(h, -1).astype(jnp.float32)

    def grid(a):
        return a.reshape(b, rows, GRID_W, h, dh).transpose(1, 0, 3, 2, 4)

    qg, kg, vg = grid(q), grid(k), grid(v)

    def band(a, rs):
        a = lax.dynamic_slice_in_dim(a, rs, wr, axis=0)[:, :, :, col_idx]
        return a.transpose(1, 2, 3, 0, 4, 5).reshape(b, h, NCB, wr * KCW, dh)

    def row_step(args):
        r, q_r = args
        rs = jnp.clip(r - wr // 2, 0, rows - wr)
        kb, vb = band(kg, rs), band(vg, rs)
        qb = q_r.reshape(b, h, NCB, QCB, dh)
        sc = jnp.einsum('bhjqd,bhjkd->bhjqk', qb, kb, preferred_element_type=jnp.float32) * (dh ** -0.5)
        dr = rs - r + key_row + (WIN_R - 1)
        bias = rpb_flat[:, dr[None, None, :] * (2 * WIN_C - 1) + dc_idx]
        sc = jnp.where(col_mask, sc + bias[None], NEG_INF)
        p = jax.nn.softmax(sc, axis=-1).astype(vb.dtype)
        return jnp.einsum('bhjqk,bhjkd->bhjqd', p, vb).reshape(b, h, GRID_W, dh)

    o = lax.map(row_step, (jnp.arange(rows, dtype=jnp.int32), qg))
    return o.transpose(1, 0, 3, 2, 4).reshape(b, s, h * dh)


def _centred_depthwise_conv(x, w):
    c = x.shape[-1]
    pad = CONV_K // 2
    return lax.conv_general_dilated(x, w.reshape(CONV_K, 1, c).astype(x.dtype), (1,), [(pad, pad)],
                                    dimension_numbers=('NWC', 'WIO', 'NWC'), feature_group_count=c)


def _chunk_gated_delta(q, k, v, g, beta):
    b, t, h, dk = q.shape
    dv = v.shape[-1]
    n = t // CHUNK

    def chunks(a):
        return a.reshape(b, n, CHUNK, h, -1).transpose(1, 0, 3, 2, 4)

    q = chunks(q) * (dk ** -0.5)
    k, v = chunks(k), chunks(v)
    g = chunks(g[..., None])[..., 0]
    beta = chunks(beta[..., None])[..., 0]
    gc = jnp.cumsum(g, axis=-1)
    incl = jnp.tril(jnp.ones((CHUNK, CHUNK), dtype=bool))
    strict = jnp.tril(jnp.ones((CHUNK, CHUNK), dtype=bool), -1)
    decay = jnp.exp(jnp.where(incl, gc[..., :, None] - gc[..., None, :], -jnp.inf))
    kb = k * beta[..., None]
    a_mat = jnp.where(strict, jnp.einsum('nbhid,nbhjd->nbhij', kb, k) * decay, 0.0)
    rhs = jnp.concatenate([v * beta[..., None], kb * jnp.exp(gc)[..., None]], axis=-1)
    sol = lax.linalg.triangular_solve(a_mat, rhs, left_side=True, lower=True, unit_diagonal=True)
    u, w = sol[..., :dv], sol[..., dv:]
    qk = jnp.einsum('nbhid,nbhjd->nbhij', q, k) * decay
    qg = q * jnp.exp(gc)[..., None]
    kg = k * jnp.exp(gc[..., -1:] - gc)[..., None]
    glast = jnp.exp(gc[..., -1])

    def step(state, inp):
        qg_c, kg_c, u_c, w_c, qk_c, gl_c = inp
        v_new = u_c - jnp.einsum('bhck,bhkv->bhcv', w_c, state)
        o = jnp.einsum('bhck,bhkv->bhcv', qg_c, state) + jnp.einsum('bhij,bhjv->bhiv', qk_c, v_new)
        state = state * gl_c[..., None, None] + jnp.einsum('bhck,bhcv->bhkv', kg_c, v_new)
        return state, o

    s0 = jnp.zeros((b, h, dk, dv), jnp.float32)
    _, o = lax.scan(step, s0, (qg, kg, u, w, qk, glast))
    return o.transpose(1, 0, 3, 2, 4).reshape(b, t, h, dv)


def _bidirectional_gated_deltanet(qkv_raw, z, beta_raw, alpha_raw, conv_w, a_log, dt_bias, onorm_w):
    b, s, _ = qkv_raw.shape
    qkv = jax.nn.silu(_centred_depthwise_conv(qkv_raw, conv_w)).astype(jnp.float32)
    q, k, v = jnp.split(qkv, [DN_KEY_WIDTH, 2 * DN_KEY_WIDTH], axis=-1)
    q = _l2norm(q.reshape(b, s, N_DN_HEADS, DN_HEAD_DIM))
    k = _l2norm(k.reshape(b, s, N_DN_HEADS, DN_HEAD_DIM))
    v = v.reshape(b, s, N_DN_HEADS, DN_HEAD_DIM)
    beta = jax.nn.sigmoid(beta_raw.astype(jnp.float32)).reshape(b, s, N_DIR, N_DN_HEADS)
    g = -jnp.exp(a_log.astype(jnp.float32)) * jax.nn.softplus(
        alpha_raw.astype(jnp.float32).reshape(b, s, N_DIR, N_DN_HEADS) + dt_bias.astype(jnp.float32))
    q2 = jnp.concatenate([q, q[:, ::-1]], axis=0)
    k2 = jnp.concatenate([k, k[:, ::-1]], axis=0)
    v2 = jnp.concatenate([v, v[:, ::-1]], axis=0)
    g2 = jnp.concatenate([g[:, :, 0], g[:, ::-1, 1]], axis=0)
    b2 = jnp.concatenate([beta[:, :, 0], beta[:, ::-1, 1]], axis=0)
    o2 = _chunk_gated_delta(q2, k2, v2, g2, b2)
    o = o2[:b] + o2[b:, ::-1]
    zh = z.astype(jnp.float32).reshape(b, s, N_DN_HEADS, DN_HEAD_DIM)
    o = _rmsnorm(o, onorm_w) * jax.nn.silu(zh)
    return o.reshape(b, s, DN_VAL_WIDTH).astype(qkv_raw.dtype)


def setup_inputs(seed: int = 0) -> dict:
    key = jax.random.key(seed)
    ks = jax.random.split(key, 16)
    f32 = jnp.float32
    nrm = lambda kk, shape, scale: jax.random.normal(kk, shape, f32) * scale
    dt = jnp.exp(jax.random.uniform(ks[5], (N_DIR, N_DN_HEADS), f32, np.log(1e-3), np.log(1e-1)))
    return {
        "x": nrm(ks[0], (BATCH, SEQ, D_MODEL), 1.0),
        "norm_mix_w": 1.0 + nrm(ks[1], (D_MODEL,), 0.02),
        "w_in": nrm(ks[2], (D_MODEL, IN_WIDTH), D_MODEL ** -0.5),
        "conv_w": nrm(ks[3], (CONV_K, DN_QKV_WIDTH), CONV_K ** -0.5),
        "a_log": jnp.log(jax.random.uniform(ks[4], (N_DIR, N_DN_HEADS), f32, 1.0, 16.0)),
        "dt_bias": dt + jnp.log(-jnp.expm1(-dt)),
        "onorm_w": 1.0 + nrm(ks[6], (DN_HEAD_DIM,), 0.02),
        "rpb": nrm(ks[7], (N_ATT_HEADS, 2 * WIN_R - 1, 2 * WIN_C - 1), 0.1),
        "w_proj_a": nrm(ks[8], (ATT_WIDTH, D_MODEL), ATT_WIDTH ** -0.5),
        "w_proj_b": nrm(ks[9], (DN_VAL_WIDTH, D_MODEL), DN_VAL_WIDTH ** -0.5),
        "w_out": nrm(ks[10], (D_MODEL, D_MODEL), D_MODEL ** -0.5),
        "norm_mlp_w": 1.0 + nrm(ks[11], (D_MODEL,), 0.02),
        "w_mlp_up": nrm(ks[12], (D_MODEL, D_FF), D_MODEL ** -0.5),
        "w_mlp_down": nrm(ks[13], (D_FF, D_MODEL), D_FF ** -0.5),
        "norm_final_w": 1.0 + nrm(ks[14], (D_MODEL,), 0.02),
    }


def reference(x, norm_mix_w, w_in, conv_w, a_log, dt_bias, onorm_w, rpb, w_proj_a, w_proj_b,
              w_out, norm_mlp_w, w_mlp_up, w_mlp_down, norm_final_w):
    b, s, _ = x.shape
    h = x
    for _ in range(DEPTH):
        xn = _rmsnorm(h, norm_mix_w)
        proj = xn @ w_in
        att_qkv, dn_qkv, dn_z, dn_beta, dn_alpha, gate_a, gate_b = jnp.split(
            proj, list(np.cumsum(IN_SPLITS)[:-1]), axis=-1)
        aq, ak, av = jnp.split(att_qkv.reshape(b, s, 3, N_ATT_HEADS, ATT_HEAD_DIM), 3, axis=2)
        y_a = _neighbourhood_attention(aq[:, :, 0], ak[:, :, 0], av[:, :, 0], rpb)
        y_b = _bidirectional_gated_deltanet(dn_qkv, dn_z, dn_beta, dn_alpha, conv_w, a_log, dt_bias, onorm_w)
        mixed = jax.nn.sigmoid(gate_a) * (y_a @ w_proj_a) + jax.nn.sigmoid(gate_b) * (y_b @ w_proj_b)
        h = h + mixed @ w_out
        hn = _rmsnorm(h, norm_mlp_w)
        h = h + jnp.square(jax.nn.relu(hn @ w_mlp_up)) @ w_mlp_down
    return _rmsnorm(h, norm_final_w)
```

```python
import functools

import numpy as np
import jax
import jax.numpy as jnp
from jax import lax
from jax.experimental import pallas as pl
from jax.experimental.pallas import tpu as pltpu

F32 = jnp.float32
BF16 = jnp.bfloat16

D_MODEL = 2048
GRID_W = 64
WIN_R = 8
WIN_C = 16
N_HEADS = 8
HEAD_DIM = 128
WIDTH = N_HEADS * HEAD_DIM
CONV_K = 5
CHUNK = 64
N_DIR = 2
D_FF = 4 * D_MODEL
RMS_EPS = 1e-6
L2_EPS = 1e-6
NEG_INF = -1e30

COL_ATT = 0
COL_DN = 3
COL_GATE_A = 6
COL_GATE_B = 8
COL_Z = 10
BIG_WIDTH = 11 * WIDTH
SMALL_WIDTH = 128

VMEM_LIMIT = 56 * 1024 * 1024


def _cparams(sem):
    return pltpu.CompilerParams(dimension_semantics=sem, vmem_limit_bytes=VMEM_LIMIT)


def _inproj_kernel(x_ref, nw_ref, wb_ref, ws_ref, ob_ref, os_ref, xn_ref, *, rows):
    j = pl.program_id(1)

    @pl.when(j == 0)
    def _():
        def body(c, carry):
            sl = pl.ds(pl.multiple_of(c * rows, rows), rows)
            x = x_ref[sl, :]
            ms = jnp.mean(x * x, axis=-1, keepdims=True)
            xn_ref[sl, :] = (x * lax.rsqrt(ms + RMS_EPS) * nw_ref[...]).astype(BF16)
            return carry

        lax.fori_loop(0, x_ref.shape[0] // rows, body, 0)
        os_ref[...] = jnp.dot(xn_ref[...], ws_ref[...], preferred_element_type=F32)

    ob_ref[...] = jnp.dot(xn_ref[...], wb_ref[...], preferred_element_type=F32).astype(BF16)


def _in_proj(x2, norm_w, w_big, w_small, tm=1024, tn=1024):
    t = x2.shape[0]
    grid = (t // tm, BIG_WIDTH // tn)
    return pl.pallas_call(
        functools.partial(_inproj_kernel, rows=128),
        grid=grid,
        in_specs=[
            pl.BlockSpec((tm, D_MODEL), lambda i, j: (i, 0)),
            pl.BlockSpec((1, D_MODEL), lambda i, j: (0, 0)),
            pl.BlockSpec((D_MODEL, tn), lambda i, j: (0, j)),
            pl.BlockSpec((D_MODEL, SMALL_WIDTH), lambda i, j: (0, 0)),
        ],
        out_specs=[
            pl.BlockSpec((tm, tn), lambda i, j: (i, j)),
            pl.BlockSpec((tm, SMALL_WIDTH), lambda i, j: (i, 0)),
        ],
        out_shape=[
            jax.ShapeDtypeStruct((t, BIG_WIDTH), BF16),
            jax.ShapeDtypeStruct((t, SMALL_WIDTH), F32),
        ],
        scratch_shapes=[pltpu.VMEM((tm, D_MODEL), BF16)],
        compiler_params=_cparams(("parallel", "arbitrary")),
        name="in_proj",
    )(x2, norm_w, w_big, w_small)


def _natt_kernel(q_ref, k_ref, v_ref, bias_ref, o_ref, *, rows_per_block, n_rows, wr):
    rb = pl.program_id(2)
    scale = HEAD_DIM ** -0.5
    nkeys = wr * GRID_W

    def body(i, carry):
        r = rb * rows_per_block + i
        rs = jnp.clip(r - wr // 2, 0, n_rows - wr)
        off = r - rs
        q = q_ref[pl.ds(pl.multiple_of(i * GRID_W, GRID_W), GRID_W), :]
        ksl = pl.ds(pl.multiple_of(rs * GRID_W, GRID_W), nkeys)
        kb = k_ref[ksl, :]
        vb = v_ref[ksl, :]
        s = lax.dot_general(q, kb, (((1,), (1,)), ((), ())), preferred_element_type=F32)
        s = s * scale + bias_ref[off]
        m = jnp.max(s, axis=-1, keepdims=True)
        p = jnp.exp(s - m)
        l = jnp.sum(p, axis=-1, keepdims=True)
        o = jnp.dot(p.astype(BF16), vb, preferred_element_type=F32)
        o_ref[pl.ds(pl.multiple_of(i * GRID_W, GRID_W), GRID_W), :] = (o / l).astype(BF16)
        return carry

    lax.fori_loop(0, rows_per_block, body, 0)


def _natt_bias_table(rpb, wr):
    qc = np.arange(GRID_W)
    cs = np.clip(qc - WIN_C // 2, 0, GRID_W - WIN_C)
    kc = np.arange(GRID_W)
    mask = (kc[None, :] >= cs[:, None]) & (kc[None, :] < cs[:, None] + WIN_C)
    dc = np.clip(kc[None, :] - qc[:, None] + WIN_C - 1, 0, 2 * WIN_C - 2)
    off = np.arange(wr)
    ki = np.arange(wr)
    dr = ki[None, :] - off[:, None] + (WIN_R - 1)
    tab = rpb.astype(F32)[:, dr[:, :, None, None], dc[None, None, :, :]]
    tab = jnp.where(mask[None, None, None], tab, NEG_INF)
    tab = tab.transpose(0, 1, 3, 2, 4)
    return tab.reshape(rpb.shape[0], wr, GRID_W, wr * GRID_W)


def _natt(big, rpb, b, s, rows_per_block=16):
    t = b * s
    n_rows = s // GRID_W
    wr = min(WIN_R, n_rows)
    rows_per_block = min(rows_per_block, n_rows)
    nrb = n_rows // rows_per_block
    bias = _natt_bias_table(rpb, wr)
    qblk = rows_per_block * GRID_W
    nb = WIDTH // HEAD_DIM
    return pl.pallas_call(
        functools.partial(_natt_kernel, rows_per_block=rows_per_block, n_rows=n_rows, wr=wr),
        grid=(b, N_HEADS, nrb),
        in_specs=[
            pl.BlockSpec((qblk, HEAD_DIM), lambda bi, h, rb: (bi * nrb + rb, COL_ATT * nb + h)),
            pl.BlockSpec((s, HEAD_DIM), lambda bi, h, rb: (bi, (COL_ATT + 1) * nb + h)),
            pl.BlockSpec((s, HEAD_DIM), lambda bi, h, rb: (bi, (COL_ATT + 2) * nb + h)),
            pl.BlockSpec((None, wr, GRID_W, wr * GRID_W), lambda bi, h, rb: (h, 0, 0, 0)),
        ],
        out_specs=pl.BlockSpec((qblk, HEAD_DIM), lambda bi, h, rb: (bi * nrb + rb, h)),
        out_shape=jax.ShapeDtypeStruct((t, WIDTH), BF16),
        compiler_params=_cparams(("parallel", "parallel", "arbitrary")),
        name="natt",
    )(big, big, big, bias)


HALO = 16


def _gdn_prep_kernel(prev_ref, cur_ref, next_ref, w_ref, o_ref, ext_ref, *, n_blocks):
    i = pl.program_id(1)
    part = pl.program_id(2)
    tc = cur_ref.shape[0]
    pad = CONV_K // 2
    prev = jnp.where(i > 0, prev_ref[...].astype(F32), 0.0)
    nxt = jnp.where(i < n_blocks - 1, next_ref[...].astype(F32), 0.0)
    ext_ref[pl.ds(0, HALO), :] = prev
    ext_ref[pl.ds(HALO, tc), :] = cur_ref[...].astype(F32)
    ext_ref[pl.ds(HALO + tc, HALO), :] = nxt
    acc = ext_ref[pl.ds(HALO - pad, tc), :] * w_ref[0:1, :]
    for kk in range(1, CONV_K):
        acc = acc + ext_ref[pl.ds(HALO - pad + kk, tc), :] * w_ref[kk:kk + 1, :]
    y = acc * jax.nn.sigmoid(acc)

    @pl.when(part < 2)
    def _():
        for h in range(N_HEADS):
            yh = y[:, h * HEAD_DIM:(h + 1) * HEAD_DIM]
            ss = jnp.sum(yh * yh, axis=-1, keepdims=True)
            o_ref[:, h * HEAD_DIM:(h + 1) * HEAD_DIM] = (yh * lax.rsqrt(ss + L2_EPS)).astype(BF16)

    @pl.when(part == 2)
    def _():
        o_ref[...] = y.astype(BF16)


def _gdn_prep(big, conv_w, b, s, tc=512):
    t = b * s
    tc = min(tc, s)
    nblk = s // tc
    hb = tc // HALO
    n_halo_blocks = t // HALO
    return pl.pallas_call(
        functools.partial(_gdn_prep_kernel, n_blocks=nblk),
        grid=(b, nblk, 3),
        in_specs=[
            pl.BlockSpec((HALO, WIDTH),
                         lambda bi, i, p: (jnp.maximum((bi * nblk + i) * hb - 1, 0), COL_DN + p)),
            pl.BlockSpec((tc, WIDTH), lambda bi, i, p: (bi * nblk + i, COL_DN + p)),
            pl.BlockSpec((HALO, WIDTH),
                         lambda bi, i, p: (jnp.minimum((bi * nblk + i + 1) * hb, n_halo_blocks - 1), COL_DN + p)),
            pl.BlockSpec((CONV_K, WIDTH), lambda bi, i, p: (0, p)),
        ],
        out_specs=pl.BlockSpec((tc, WIDTH), lambda bi, i, p: (bi * nblk + i, p)),
        out_shape=jax.ShapeDtypeStruct((t, 3 * WIDTH), BF16),
        scratch_shapes=[pltpu.VMEM((tc + 2 * HALO, WIDTH), F32)],
        compiler_params=_cparams(("parallel", "parallel", "arbitrary")),
        name="gdn_prep",
    )(big, big, big, conv_w)


def _split_dot(a_01, x):
    hi = x.astype(BF16)
    lo = (x - hi.astype(F32)).astype(BF16)
    a = a_01.astype(BF16)
    return (jnp.dot(a, hi, preferred_element_type=F32) + jnp.dot(a, lo, preferred_element_type=F32))


def _mm(a, b):
    return jnp.dot(a, b, preferred_element_type=F32)


def _mm_nt(a, b):
    return lax.dot_general(a, b, (((1,), (1,)), ((), ())), preferred_element_type=F32)


def _mm_tn(a, b):
    return lax.dot_general(a, b, (((0,), (0,)), ((), ())), preferred_element_type=F32)


def _gdn_direction(d, qkv_ref, sm_ref, alog_ref, dtb_ref, o_ref, s_ref):
    c = CHUNK
    ri = lax.broadcasted_iota(jnp.int32, (c, c), 0)
    ci = lax.broadcasted_iota(jnp.int32, (c, c), 1)
    if d == 0:
        incl = ci <= ri
        strict = ci < ri
        last = c - 1
    else:
        incl = ci >= ri
        strict = ci > ri
        last = 0
    sm = sm_ref[...]
    beta_all = jax.nn.sigmoid(sm)
    z = sm + dtb_ref[...]
    softplus = jnp.maximum(z, 0.0) + jnp.log1p(jnp.exp(-jnp.abs(z)))
    g_all = -jnp.exp(alog_ref[...]) * softplus
    gc_all = _split_dot(incl.astype(F32), g_all)
    gc_last = gc_all[last:last + 1, :]
    eq_all = jnp.exp(gc_all)
    ek_all = jnp.exp(gc_last - gc_all)
    gl_all = jnp.exp(gc_last)
    gct = jnp.concatenate([gc_all, jnp.zeros_like(gc_all)], axis=0).T

    eye = (ri == ci).astype(F32)
    scale = HEAD_DIM ** -0.5
    for h in range(N_HEADS):
        cb = d * N_HEADS + h
        cg = 2 * N_HEADS + cb
        q = qkv_ref[:, h * HEAD_DIM:(h + 1) * HEAD_DIM].astype(F32) * scale
        k = qkv_ref[:, WIDTH + h * HEAD_DIM:WIDTH + (h + 1) * HEAD_DIM].astype(F32)
        v = qkv_ref[:, 2 * WIDTH + h * HEAD_DIM:2 * WIDTH + (h + 1) * HEAD_DIM].astype(F32)
        beta = jnp.broadcast_to(beta_all[:, cb:cb + 1], (c, HEAD_DIM))
        eq = jnp.broadcast_to(eq_all[:, cg:cg + 1], (c, HEAD_DIM))
        ek = jnp.broadcast_to(ek_all[:, cg:cg + 1], (c, HEAD_DIM))
        gcol = jnp.broadcast_to(gc_all[:, cg:cg + 1], (c, c))
        grow = jnp.broadcast_to(gct[cg:cg + 1, 0:c], (c, c))
        diff = gcol - grow
        decay = jnp.exp(jnp.where(incl, diff, NEG_INF))
        kb = k * beta
        a_mat = jnp.where(strict, _mm_nt(kb, k) * decay, 0.0)
        pw = -a_mat
        tinv = eye + pw
        for _ in range(5):
            pw = _mm(pw, pw)
            tinv = tinv + _mm(tinv, pw)
        rhs = jnp.concatenate([v * beta, kb * eq], axis=1)
        sol = _mm(tinv, rhs)
        u = sol[:, :HEAD_DIM]
        w = sol[:, HEAD_DIM:]
        qk = _mm_nt(q, k) * decay
        qg = q * eq
        kg = k * ek
        state = s_ref[d, h]
        v_new = u - _mm(w, state)
        o = _mm(qg, state) + _mm(qk, v_new)
        gl = jnp.broadcast_to(gl_all[0:1, cg:cg + 1], (HEAD_DIM, HEAD_DIM))
        s_ref[d, h] = state * gl + _mm_tn(kg, v_new)
        o_ref[:, h * HEAD_DIM:(h + 1) * HEAD_DIM] = o.astype(BF16)


def _gdn_scan_kernel(qkv_f_ref, qkv_b_ref, sm_f_ref, sm_b_ref, alog_ref, dtb_ref, of_ref, ob_ref, s_ref):
    @pl.when(pl.program_id(1) == 0)
    def _():
        s_ref[...] = jnp.zeros_like(s_ref)

    _gdn_direction(0, qkv_f_ref, sm_f_ref, alog_ref, dtb_ref, of_ref, s_ref)
    _gdn_direction(1, qkv_b_ref, sm_b_ref, alog_ref, dtb_ref, ob_ref, s_ref)


def _gdn_scan(qkv_c, small, alog_row, dtb_row, b, s):
    t = b * s
    n = s // CHUNK
    fwd = lambda bi, ti: (bi * n + ti, 0)
    bwd = lambda bi, ti: (bi * n + (n - 1 - ti), 0)
    return pl.pallas_call(
        _gdn_scan_kernel,
        grid=(b, n),
        in_specs=[
            pl.BlockSpec((CHUNK, 3 * WIDTH), fwd),
            pl.BlockSpec((CHUNK, 3 * WIDTH), bwd),
            pl.BlockSpec((CHUNK, SMALL_WIDTH), fwd),
            pl.BlockSpec((CHUNK, SMALL_WIDTH), bwd),
            pl.BlockSpec((1, SMALL_WIDTH), lambda bi, ti: (0, 0)),
            pl.BlockSpec((1, SMALL_WIDTH), lambda bi, ti: (0, 0)),
        ],
        out_specs=[pl.BlockSpec((CHUNK, WIDTH), fwd), pl.BlockSpec((CHUNK, WIDTH), bwd)],
        out_shape=[jax.ShapeDtypeStruct((t, WIDTH), BF16), jax.ShapeDtypeStruct((t, WIDTH), BF16)],
        scratch_shapes=[pltpu.VMEM((N_DIR, N_HEADS, HEAD_DIM, HEAD_DIM), F32)],
        compiler_params=_cparams(("parallel", "arbitrary")),
        name="gdn_scan",
    )(qkv_c, qkv_c, small, small, alog_row, dtb_row)


def _mix_kernel(ya_ref, of_ref, ob_ref, z_ref, ga_ref, gb_ref, x_ref, onw_ref, wpa_ref, wpb_ref, wout_ref,
                h_ref, yb_ref):
    for h in range(N_HEADS):
        sl = slice(h * HEAD_DIM, (h + 1) * HEAD_DIM)
        o = of_ref[:, sl].astype(F32) + ob_ref[:, sl].astype(F32)
        ms = jnp.mean(o * o, axis=-1, keepdims=True)
        z = z_ref[:, sl].astype(F32)
        yb = o * lax.rsqrt(ms + RMS_EPS) * onw_ref[...] * (z * jax.nn.sigmoid(z))
        yb_ref[:, sl] = yb.astype(BF16)
    pa = jnp.dot(ya_ref[...], wpa_ref[...], preferred_element_type=F32)
    pb = jnp.dot(yb_ref[...], wpb_ref[...], preferred_element_type=F32)
    mixed = (jax.nn.sigmoid(ga_ref[...].astype(F32)) * pa + jax.nn.sigmoid(gb_ref[...].astype(F32)) * pb)
    h_ref[...] = x_ref[...] + jnp.dot(mixed.astype(BF16), wout_ref[...], preferred_element_type=F32)


def _const_spec(shape):
    return pl.BlockSpec(shape, lambda i: (0,) * len(shape), pipeline_mode=pl.Buffered(1))


def _mix(ya, o_f, o_b, big, x2, onorm_row, wpa, wpb, wout, tm=512):
    t = x2.shape[0]
    return pl.pallas_call(
        _mix_kernel,
        grid=(t // tm,),
        in_specs=[
            pl.BlockSpec((tm, WIDTH), lambda i: (i, 0)),
            pl.BlockSpec((tm, WIDTH), lambda i: (i, 0)),
            pl.BlockSpec((tm, WIDTH), lambda i: (i, 0)),
            pl.BlockSpec((tm, WIDTH), lambda i: (i, COL_Z)),
            pl.BlockSpec((tm, D_MODEL), lambda i: (i, COL_GATE_A // 2)),
            pl.BlockSpec((tm, D_MODEL), lambda i: (i, COL_GATE_B // 2)),
            pl.BlockSpec((tm, D_MODEL), lambda i: (i, 0)),
            _const_spec((1, HEAD_DIM)),
            _const_spec((WIDTH, D_MODEL)),
            _const_spec((WIDTH, D_MODEL)),
            _const_spec((D_MODEL, D_MODEL)),
        ],
        out_specs=pl.BlockSpec((tm, D_MODEL), lambda i: (i, 0)),
        out_shape=jax.ShapeDtypeStruct((t, D_MODEL), F32),
        scratch_shapes=[pltpu.VMEM((tm, WIDTH), BF16)],
        compiler_params=_cparams(("parallel",)),
        name="mix",
    )(ya, o_f, o_b, big, big, big, x2, onorm_row, wpa, wpb, wout)


def _mlp_kernel(h_ref, nw_ref, wup_ref, wdn_ref, fw_ref, y_ref, hn_ref, acc_ref, *, rows):
    f = pl.program_id(1)
    nrow = h_ref.shape[0] // rows

    @pl.when(f == 0)
    def _():
        def body(c, carry):
            sl = pl.ds(pl.multiple_of(c * rows, rows), rows)
            x = h_ref[sl, :]
            ms = jnp.mean(x * x, axis=-1, keepdims=True)
            hn_ref[sl, :] = (x * lax.rsqrt(ms + RMS_EPS) * nw_ref[...]).astype(BF16)
            acc_ref[sl, :] = x
            return carry

        lax.fori_loop(0, nrow, body, 0)

    up = jnp.dot(hn_ref[...], wup_ref[...], preferred_element_type=F32)
    act = jnp.square(jnp.maximum(up, 0.0)).astype(BF16)
    acc_ref[...] += jnp.dot(act, wdn_ref[...], preferred_element_type=F32)

    @pl.when(f == pl.num_programs(1) - 1)
    def _():
        def body(c, carry):
            sl = pl.ds(pl.multiple_of(c * rows, rows), rows)
            x = acc_ref[sl, :]
            ms = jnp.mean(x * x, axis=-1, keepdims=True)
            y_ref[sl, :] = x * lax.rsqrt(ms + RMS_EPS) * fw_ref[...]
            return carry

        lax.fori_loop(0, nrow, body, 0)


def _mlp(h2, norm_w, wup, wdn, final_w, tm=512, tf=512):
    t = h2.shape[0]
    return pl.pallas_call(
        functools.partial(_mlp_kernel, rows=128),
        grid=(t // tm, D_FF // tf),
        in_specs=[
            pl.BlockSpec((tm, D_MODEL), lambda i, f: (i, 0)),
            pl.BlockSpec((1, D_MODEL), lambda i, f: (0, 0)),
            pl.BlockSpec((D_MODEL, tf), lambda i, f: (0, f)),
            pl.BlockSpec((tf, D_MODEL), lambda i, f: (f, 0)),
            pl.BlockSpec((1, D_MODEL), lambda i, f: (0, 0)),
        ],
        out_specs=pl.BlockSpec((tm, D_MODEL), lambda i, f: (i, 0)),
        out_shape=jax.ShapeDtypeStruct((t, D_MODEL), F32),
        scratch_shapes=[pltpu.VMEM((tm, D_MODEL), BF16), pltpu.VMEM((tm, D_MODEL), F32)],
        compiler_params=_cparams(("parallel", "arbitrary")),
        name="mlp",
    )(h2, norm_w, wup, wdn, final_w)


def kernel(x, norm_mix_w, w_in, conv_w, a_log, dt_bias, onorm_w, rpb, w_proj_a, w_proj_b, w_out, norm_mlp_w,
           w_mlp_up, w_mlp_down, norm_final_w):
    b, s, d = x.shape
    t = b * s
    x2 = x.reshape(t, d)

    att_w = 3 * WIDTH
    dn_w = 3 * WIDTH
    o_z = att_w + dn_w
    o_beta = o_z + WIDTH
    o_ga = o_beta + 2 * N_DIR * N_HEADS
    o_gb = o_ga + D_MODEL
    w_big = jnp.concatenate(
        [w_in[:, :o_z], w_in[:, o_ga:o_gb], w_in[:, o_gb:], w_in[:, o_z:o_beta]], axis=1).astype(BF16)
    w_small = jnp.pad(w_in[:, o_beta:o_ga], ((0, 0), (0, SMALL_WIDTH - (o_ga - o_beta)))).astype(BF16)
    pad_row = lambda a: jnp.pad(a.reshape(1, -1).astype(F32),
                                ((0, 0), (2 * N_HEADS, SMALL_WIDTH - 2 * N_HEADS - a.size)))
    alog_row = pad_row(a_log)
    dtb_row = pad_row(dt_bias)

    big, small = _in_proj(x2, norm_mix_w.reshape(1, d), w_big, w_small)
    y_a = _natt(big, rpb, b, s)
    qkv_c = _gdn_prep(big, conv_w.astype(F32), b, s)
    o_f, o_b = _gdn_scan(qkv_c, small, alog_row, dtb_row, b, s)
    h = _mix(y_a, o_f, o_b, big, x2, onorm_w.reshape(1, HEAD_DIM).astype(F32),
             w_proj_a.astype(BF16), w_proj_b.astype(BF16), w_out.astype(BF16))
    y = _mlp(h, norm_mlp_w.reshape(1, d), w_mlp_up.astype(BF16), w_mlp_down.astype(BF16),
             norm_final_w.reshape(1, d))
    return y.reshape(b, s, d)
```
